```python
import math
import jax, jax.numpy as jnp
from jax import lax
import numpy as np

D_MODEL = 2048
BATCH = 1
SEQ = 16384
DEPTH = 4

GRID_W = 64
CTX_LEN = 256
EPS = 1e-6
F32 = jnp.float32

A_HEADS = 8
A_HEAD_DIM = 64
A_V_DIM = 2 * A_HEAD_DIM
A_QK = A_HEADS * 2 * A_HEAD_DIM
A_WIDTH = A_HEADS * A_V_DIM
B_GROUPS = 4
B_WIDTH = D_MODEL - A_WIDTH
B_GROUP_DIM = B_WIDTH // B_GROUPS
POOL_WINDOWS = (2, 4, 8, 16)
EVEN_IN = 2 * A_QK + A_WIDTH + B_WIDTH
ROPE_THETA = 10000.0
Q_BLOCK = 128

C_HEADS = 4
C_DK = D_MODEL // 2 // C_HEADS
C_DV = D_MODEL // C_HEADS
C_NQ = C_HEADS * C_DK
C_NV = C_HEADS * C_DV
C_GATE_RANK = 16
C_TAU = 16.0
C_CHUNK = 64
ODD_IN = 2 * C_NQ + 2 * C_NV + 2 * C_GATE_RANK

D_FF = 5632
N_EXPERTS = 8
TOP_K = 2
EXPERT_FF = 5632

kernel_name = 'hybrid_diffpool_gla_moe_diffusion_trunk'


def rmsnorm(x, g):
    xf = x.astype(F32)
    y = xf * lax.rsqrt(jnp.mean(xf * xf, axis=-1, keepdims=True) + EPS)
    return (y * g.astype(F32)).astype(x.dtype)


def modulate(h, shift, scale):
    return h * (1.0 + scale) + shift


def axial_rope_tables(n_tokens):
    rows = n_tokens // GRID_W
    row = jnp.repeat(jnp.arange(rows), GRID_W).astype(F32)
    col = jnp.tile(jnp.arange(GRID_W), rows).astype(F32)
    n_freq = A_HEAD_DIM // 4
    inv = ROPE_THETA ** (-jnp.arange(n_freq, dtype=F32) / n_freq)
    ar = row[:, None] * inv
    ac = col[:, None] * inv
    return (jnp.cos(ar), jnp.sin(ar), jnp.cos(ac), jnp.sin(ac))


def _rot(x, cos, sin):
    x1, x2 = jnp.split(x, 2, axis=-1)
    return jnp.concatenate([x1 * cos - x2 * sin, x2 * cos + x1 * sin], axis=-1)


def apply_axial_rope(x, tabs):
    cr, sr, cc, sc = (t.astype(x.dtype) for t in tabs)
    xr, xc = jnp.split(x, 2, axis=-1)
    return jnp.concatenate([_rot(xr, cr, sr), _rot(xc, cc, sc)], axis=-1)


def diff_core(q1, q2, k1, k2, v, lam):
    scale = A_HEAD_DIM ** -0.5
    p1 = jax.nn.softmax(jnp.einsum('bhqd,bhkd->bhqk', q1, k1).astype(F32) * scale, axis=-1)
    p2 = jax.nn.softmax(jnp.einsum('bhqd,bhkd->bhqk', q2, k2).astype(F32) * scale, axis=-1)
    w = (p1 - lam * p2).astype(v.dtype)
    return jnp.einsum('bhqk,bhkv->bhqv', w, v)


def diff_attn_blocks(q1, q2, k1, k2, v, lam):
    b, h, n, dh = q1.shape
    nb = n // Q_BLOCK
    def blocks(a):
        return jnp.moveaxis(a.reshape(b, h, nb, Q_BLOCK, dh), 2, 0)
    o = lax.map(lambda qs: diff_core(qs[0], qs[1], k1, k2, v, lam), (blocks(q1), blocks(q2)))
    return jnp.moveaxis(o, 0, 2).reshape(b, h, n, v.shape[-1])


def multiscale_pool(u, w_pool, pool_scale):
    b, n, _ = u.shape
    uf = u.astype(F32).reshape(b, n, B_GROUPS, B_GROUP_DIM)
    csum = jnp.concatenate([jnp.zeros((b, 1, B_GROUPS, B_GROUP_DIM), F32), jnp.cumsum(uf, axis=1)], axis=1)
    t = jnp.arange(n)
    outs = []
    for gi, w in enumerate(POOL_WINDOWS):
        lo = jnp.clip(t - w // 2, 0, n - 1)
        hi = jnp.clip(t + (w - 1 - w // 2), 0, n - 1)
        s = csum[:, hi + 1, gi] - csum[:, lo, gi]
        cnt = (hi - lo + 1).astype(F32)[None, :, None]
        outs.append(s / cnt - uf[:, :, gi])
    pooled = jnp.stack(outs, axis=2).astype(u.dtype)
    mixed = jnp.einsum('blgc,gcd->blgd', pooled, w_pool).reshape(b, n, B_WIDTH)
    return mixed * pool_scale


def diff_pool_mixer(h_lat, h_ctx, w_in, w_out, lam_p, subln_g, w_pool, pool_scale, lam_init, rope, with_ctx):
    def split(p):
        b, n, _ = p.shape
        qa, ka, va, ub = jnp.split(p, [A_QK, 2 * A_QK, 2 * A_QK + A_WIDTH], axis=-1)
        q = qa.reshape(b, n, A_HEADS, 2, A_HEAD_DIM).transpose(3, 0, 2, 1, 4)
        k = ka.reshape(b, n, A_HEADS, 2, A_HEAD_DIM).transpose(3, 0, 2, 1, 4)
        v = va.reshape(b, n, A_HEADS, A_V_DIM).transpose(0, 2, 1, 3)
        return q, k, v, ub

    lp = lam_p.astype(F32)
    lam = jnp.exp(jnp.sum(lp[0] * lp[1])) - jnp.exp(jnp.sum(lp[2] * lp[3])) + lam_init

    def finish_attn(o):
        b, _, n, _ = o.shape
        o = rmsnorm(o, subln_g) * (1.0 - lam_init)
        return o.transpose(0, 2, 1, 3).reshape(b, n, A_WIDTH)

    q_l, k_l, v_l, u_l = split(h_lat @ w_in)
    q_c, k_c, v_c, u_c = split(h_ctx @ w_in)
    k1 = jnp.concatenate([apply_axial_rope(k_l[0], rope), k_c[0]], axis=2)
    k2 = jnp.concatenate([apply_axial_rope(k_l[1], rope), k_c[1]], axis=2)
    vv = jnp.concatenate([v_l, v_c], axis=2)
    a_lat = diff_attn_blocks(apply_axial_rope(q_l[0], rope), apply_axial_rope(q_l[1], rope), k1, k2, vv, lam)
    out_lat = jnp.concatenate([finish_attn(a_lat), multiscale_pool(u_l, w_pool, pool_scale)], axis=-1) @ w_out
    if not with_ctx:
        return out_lat, None
    a_ctx = diff_core(q_c[0], q_c[1], k_c[0], k_c[1], v_c, lam)
    out_ctx = jnp.concatenate([finish_attn(a_ctx), multiscale_pool(u_c, w_pool, pool_scale)], axis=-1) @ w_out
    return out_lat, out_ctx


def gla_chunk_scan(q, k, v, g, s0, strict):
    b, h, n, dk = q.shape
    dv = v.shape[-1]
    nc = n // C_CHUNK
    def chunks(a):
        return jnp.moveaxis(a.reshape(b, h, nc, C_CHUNK, a.shape[-1]), 2, 0)
    mask = jnp.tril(jnp.ones((C_CHUNK, C_CHUNK), dtype=bool), k=-1 if strict else 0)

    def step(s, inp):
        qn, kn, vn, gn = (a.astype(F32) for a in inp)
        bc = jnp.cumsum(gn, axis=-2)
        b_last = bc[..., -1:, :]
        q_dec = qn * jnp.exp(bc)
        k_inv = kn * jnp.exp(-bc)
        att = jnp.where(mask, jnp.einsum('bhid,bhjd->bhij', q_dec, k_inv), 0.0)
        o = jnp.einsum('bhij,bhjv->bhiv', att, vn) + jnp.einsum('bhid,bhdv->bhiv', q_dec, s)
        k_end = kn * jnp.exp(b_last - bc)
        s_new = jnp.exp(b_last[..., 0, :])[..., None] * s + jnp.einsum('bhjd,bhjv->bhdv', k_end, vn)
        return s_new, o

    s_fin, oc = lax.scan(step, s0, (chunks(q), chunks(k), chunks(v), chunks(g)))
    o = jnp.moveaxis(oc, 0, 2).reshape(b, h, n, dv)
    return o.astype(v.dtype), s_fin


def gla_mixer(h_lat, h_ctx, w_in, gate_w2, gate_b, norm_g, w_out, with_ctx):
    def heads(a, d):
        b, n, _ = a.shape
        return a.reshape(b, n, C_HEADS, d).transpose(0, 2, 1, 3)

    def split(p):
        q, k, v, r, gl = jnp.split(p, [C_NQ, 2 * C_NQ, 2 * C_NQ + C_NV, 2 * C_NQ + 2 * C_NV], axis=-1)
        gates = []
        for d in range(2):
            z = gl[..., d * C_GATE_RANK:(d + 1) * C_GATE_RANK] @ gate_w2[d] + gate_b[d]
            gates.append(heads(jax.nn.log_sigmoid(z.astype(F32)) / C_TAU, C_DK))
        return heads(q, C_DK) * (C_DK ** -0.5), heads(k, C_DK), heads(v, C_DV), r, gates[0], gates[1]

    def flip(a):
        return jnp.flip(a, axis=2)

    def finish(o, r):
        b, _, n, _ = o.shape
        o = rmsnorm(o.transpose(0, 2, 1, 3), norm_g.reshape(C_HEADS, C_DV)).reshape(b, n, C_NV)
        return (o * jax.nn.silu(r)) @ w_out

    qc, kc, vc, rc, gfc, gbc = split(h_ctx @ w_in)
    ql, kl, vl, rl, gfl, gbl = split(h_lat @ w_in)
    s0 = jnp.zeros((h_ctx.shape[0], C_HEADS, C_DK, C_DV), F32)
    o_cf, s_cf = gla_chunk_scan(qc, kc, vc, gfc, s0, False)
    o_cb, s_cb = gla_chunk_scan(flip(qc), flip(kc), flip(vc), flip(gbc), s0, True)
    o_lf, _ = gla_chunk_scan(ql, kl, vl, gfl, s_cf, False)
    o_lb, _ = gla_chunk_scan(flip(ql), flip(kl), flip(vl), flip(gbl), s_cb, True)
    out_lat = finish(o_lf + flip(o_lb), rl)
    if not with_ctx:
        return out_lat, None
    return out_lat, finish(o_cf + flip(o_cb), rc)


def swiglu(h, wg, wu, wd):
    return (jax.nn.silu(h @ wg) * (h @ wu)) @ wd


def moe_swiglu(h, w_router, wg, wu, wd):
    b, n, d = h.shape
    t = h.reshape(b * n, d)
    logits = (t @ w_router).astype(F32)
    top_v, top_i = lax.top_k(logits, TOP_K)
    top_w = jax.nn.softmax(top_v, axis=-1)
    gates = jnp.sum(jax.nn.one_hot(top_i, N_EXPERTS, dtype=F32) * top_w[..., None], axis=1).astype(t.dtype)
    y = jnp.zeros_like(t)
    for e in range(N_EXPERTS):
        y = y + gates[:, e:e + 1] * swiglu(t, wg[e], wu[e], wd[e])
    return y.reshape(b, n, d)


def setup_inputs(seed: int = 0) -> dict:
    key = jax.random.key(seed)
    ks = jax.random.split(key, 32)
    n_even = (DEPTH + 1) // 2
    n_odd = DEPTH // 2
    D = D_MODEL
    def nrm(k, shape, scale):
        return jax.random.normal(k, shape, F32) * scale
    return {
        'x': nrm(ks[0], (BATCH, SEQ, D), 1.0),
        'c': nrm(ks[1], (BATCH, D), 1.0),
        'ctx': nrm(ks[2], (BATCH, CTX_LEN, D), 1.0),
        'c_ctx': nrm(ks[3], (D,), 1.0),
        'w_mod': nrm(ks[4], (DEPTH, D, 6 * D), 0.5 * D ** -0.5),
        'b_mod': nrm(ks[5], (DEPTH, 6 * D), 0.01),
        'norm_g': 1.0 + nrm(ks[6], (DEPTH, 4, D), 0.05),
        'w_in_a': nrm(ks[7], (n_even, D, EVEN_IN), D ** -0.5),
        'w_out_a': nrm(ks[8], (n_even, A_WIDTH + B_WIDTH, D), (A_WIDTH + B_WIDTH) ** -0.5),
        'diff_lambda': nrm(ks[9], (n_even, 4, A_HEAD_DIM), 0.1),
        'subln_g': 1.0 + nrm(ks[10], (n_even, A_V_DIM), 0.05),
        'w_pool': nrm(ks[11], (n_even, B_GROUPS, B_GROUP_DIM, B_GROUP_DIM), B_GROUP_DIM ** -0.5),
        'pool_scale': 1.0 + nrm(ks[12], (n_even, B_WIDTH), 0.1),
        'w_in_c': nrm(ks[13], (n_odd, D, ODD_IN), D ** -0.5),
        'gate_w2': nrm(ks[14], (n_odd, 2, C_GATE_RANK, C_NQ), C_GATE_RANK ** -0.5),
        'gate_b': nrm(ks[15], (n_odd, 2, C_NQ), 0.1),
        'gla_norm': 1.0 + nrm(ks[16], (n_odd, C_NV), 0.05),
        'w_out_c': nrm(ks[17], (n_odd, C_NV, D), C_NV ** -0.5),
        'w_ff_gate': nrm(ks[18], (n_even, D, D_FF), D ** -0.5),
        'w_ff_up': nrm(ks[19], (n_even, D, D_FF), D ** -0.5),
        'w_ff_down': nrm(ks[20], (n_even, D_FF, D), D_FF ** -0.5),
        'w_router': nrm(ks[21], (n_odd, D, N_EXPERTS), D ** -0.5),
        'w_moe_gate': nrm(ks[22], (n_odd, N_EXPERTS, D, EXPERT_FF), D ** -0.5),
        'w_moe_up': nrm(ks[23], (n_odd, N_EXPERTS, D, EXPERT_FF), D ** -0.5),
        'w_moe_down': nrm(ks[24], (n_odd, N_EXPERTS, EXPERT_FF, D), EXPERT_FF ** -0.5),
    }


def reference(x, c, ctx, c_ctx, w_mod, b_mod, norm_g, w_in_a, w_out_a, diff_lambda, subln_g, w_pool, pool_scale, w_in_c, gate_w2, gate_b, gla_norm, w_out_c, w_ff_gate, w_ff_up, w_ff_down, w_router, w_moe_gate, w_moe_up, w_moe_down):
    rope = axial_rope_tables(x.shape[1])
    cond_lat = jax.nn.silu(c)[:, None, :]
    cond_ctx = jax.nn.silu(c_ctx)
    for l in range(DEPTH):
        last = l == DEPTH - 1
        i = l // 2
        mod_l = jnp.split(cond_lat @ w_mod[l] + b_mod[l], 6, axis=-1)
        mod_c = jnp.split(cond_ctx @ w_mod[l] + b_mod[l], 6, axis=-1)
        g = norm_g[l]
        h = modulate(rmsnorm(x, g[0]), mod_l[0], mod_l[1])
        hc = modulate(rmsnorm(ctx, g[0]), mod_c[0], mod_c[1])
        if l % 2 == 0:
            lam_init = 0.8 - 0.6 * math.exp(-0.3 * l)
            m, mc = diff_pool_mixer(h, hc, w_in_a[i], w_out_a[i], diff_lambda[i], subln_g[i], w_pool[i], pool_scale[i], lam_init, rope, not last)
            ffn = lambda t: swiglu(t, w_ff_gate[i], w_ff_up[i], w_ff_down[i])
        else:
            m, mc = gla_mixer(h, hc, w_in_c[i], gate_w2[i], gate_b[i], gla_norm[i], w_out_c[i], not last)
            ffn = lambda t: moe_swiglu(t, w_router[i], w_moe_gate[i], w_moe_up[i], w_moe_down[i])
        x = x + mod_l[2] * rmsnorm(m, g[1])
        h = modulate(rmsnorm(x, g[2]), mod_l[3], mod_l[4])
        x = x + mod_l[5] * rmsnorm(ffn(h), g[3])
        if not last:
            ctx = ctx + mod_c[2] * rmsnorm(mc, g[1])
            hc = modulate(rmsnorm(ctx, g[2]), mod_c[3], mod_c[4])
            ctx = ctx + mod_c[5] * rmsnorm(ffn(hc), g[3])
    return x
```

```python
import functools
import math

import jax
import jax.numpy as jnp
from jax import lax
from jax.experimental import pallas as pl
from jax.experimental.pallas import tpu as pltpu

F32 = jnp.float32
BF16 = jnp.bfloat16
EPS = 1e-6

LANES = 128
VMEM_LIMIT_BYTES = 56 * 2**20

GRID_W = 64
A_HEADS = 8
A_HEAD_DIM = 64
A_V_DIM = 2 * A_HEAD_DIM
A_QK = A_HEADS * 2 * A_HEAD_DIM
A_WIDTH = A_HEADS * A_V_DIM
B_GROUPS = 4
B_GROUP_DIM = 256
B_WIDTH = B_GROUPS * B_GROUP_DIM
POOL_WINDOWS = (2, 4, 8, 16)
POOL_HALO = 16
ROPE_THETA = 10000.0
C_HEADS = 4
C_DK = 256
C_DV = 512
C_NQ = C_HEADS * C_DK
C_NV = C_HEADS * C_DV
C_GATE_RANK = 16
C_TAU = 16.0
C_CHUNK = 64
GLA_ROWS = 256
N_EXPERTS = 8
TOP_K = 2


def _cp(*sem):
    return pltpu.CompilerParams(dimension_semantics=sem, vmem_limit_bytes=VMEM_LIMIT_BYTES)


def _tile(n, pref, mult=8):
    t = min(pref, n)
    t -= t % mult
    while t > mult and n % t:
        t -= mult
    assert t > 0 and n % t == 0, (n, pref, mult)
    return t


def _dot(a, b):
    return jnp.dot(a, b, preferred_element_type=F32)


def _dot_nt(a, b):
    return lax.dot_general(a, b, (((1,), (1,)), ((), ())), preferred_element_type=F32)


def _dot_tn(a, b):
    return lax.dot_general(a, b, (((0,), (0,)), ((), ())), preferred_element_type=F32)


def _split3(a):
    hi = a.astype(BF16)
    r = a - hi.astype(F32)
    mid = r.astype(BF16)
    lo = (r - mid.astype(F32)).astype(BF16)
    return hi, mid, lo


def _dot_f32(a, b):
    a0, a1, a2 = _split3(a)
    b0, b1, b2 = _split3(b)
    return (_dot(a0, b0) + (_dot(a0, b1) + _dot(a1, b0))
            + (_dot(a0, b2) + _dot(a1, b1) + _dot(a2, b0)))


def _rms(x, g):
    return x * lax.rsqrt(jnp.mean(x * x, axis=-1, keepdims=True) + EPS) * g


def _silu(x):
    return x * (1.0 / (1.0 + jnp.exp(-x)))


def _norm_mod(x, g, shift, scale):
    return _rms(x, g) * (1.0 + scale) + shift


def _modvec_kernel(c_ref, w_ref, b_ref, o_ref):
    o_ref[0] = _dot_f32(_silu(c_ref[...]), w_ref[0]) + b_ref[0]


def _modvec(cond, w_mod, b_mod):
    depth, d, n6 = w_mod.shape
    rows = cond.shape[0]
    tn = _tile(n6, 512, LANES)
    return pl.pallas_call(
        _modvec_kernel,
        grid=(depth, n6 // tn),
        in_specs=[
            pl.BlockSpec((rows, d), lambda l, j: (0, 0)),
            pl.BlockSpec((1, d, tn), lambda l, j: (l, 0, j)),
            pl.BlockSpec((1, 1, tn), lambda l, j: (l, 0, j)),
        ],
        out_specs=pl.BlockSpec((1, rows, tn), lambda l, j: (l, 0, j)),
        out_shape=jax.ShapeDtypeStruct((depth, rows, n6), F32),
        compiler_params=_cp("parallel", "parallel"),
        name="modvec",
    )(cond, w_mod, b_mod.reshape(depth, 1, n6))


def _proj_kernel(*refs, rope_tiles):
    if rope_tiles:
        x_ref, g_ref, sh_ref, sc_ref, w_ref, cos_ref, sin_ref, o_ref, h_scr = refs
    else:
        x_ref, g_ref, sh_ref, sc_ref, w_ref, o_ref, h_scr = refs
    j = pl.program_id(1)

    @pl.when(j == 0)
    def _():
        h_scr[...] = _norm_mod(x_ref[...], g_ref[...], sh_ref[...], sc_ref[...]).astype(BF16)

    acc = _dot(h_scr[...], w_ref[...])
    if not rope_tiles:
        o_ref[...] = acc.astype(o_ref.dtype)
        return

    @pl.when(j < rope_tiles)
    def _():
        tn = acc.shape[1]
        reps = tn // LANES
        cos = jnp.concatenate([cos_ref[...]] * reps, axis=1)
        sin = jnp.concatenate([sin_ref[...]] * reps, axis=1)
        lane = lax.broadcasted_iota(jnp.int32, acc.shape, 1)
        partner = jnp.where((lane & 16) == 0, pltpu.roll(acc, tn - 16, 1), pltpu.roll(acc, 16, 1))
        o_ref[...] = (acc * cos + partner * sin).astype(o_ref.dtype)

    @pl.when(j >= rope_tiles)
    def _():
        o_ref[...] = acc.astype(o_ref.dtype)


def _proj(x, g, shift, scale, w, rope=None, rope_cols=0, tn_pref=1024):
    m, d = x.shape
    n = w.shape[1]
    tm = _tile(m, 512)
    tn = _tile(n, tn_pref, LANES)
    vec = pl.BlockSpec((1, d), lambda i, j: (0, 0))
    in_specs = [pl.BlockSpec((tm, d), lambda i, j: (i, 0)), vec, vec, vec,
                pl.BlockSpec((d, tn), lambda i, j: (0, j))]
    args = [x, g, shift, scale, w]
    rope_tiles = 0
    if rope is not None:
        assert rope_cols % tn == 0
        rope_tiles = rope_cols // tn
        tab = pl.BlockSpec((tm, LANES), lambda i, j: (i, 0))
        in_specs += [tab, tab]
        args += list(rope)
    return pl.pallas_call(
        functools.partial(_proj_kernel, rope_tiles=rope_tiles),
        grid=(m // tm, n // tn),
        in_specs=in_specs,
        out_specs=pl.BlockSpec((tm, tn), lambda i, j: (i, j)),
        out_shape=jax.ShapeDtypeStruct((m, n), BF16),
        scratch_shapes=[pltpu.VMEM((tm, d), BF16)],
        compiler_params=_cp("parallel", "arbitrary"),
        name="proj",
    )(*args)


def _rope_tables(n):
    t = jnp.arange(n)
    row = (t // GRID_W).astype(F32)
    col = (t % GRID_W).astype(F32)
    n_freq = A_HEAD_DIM // 4
    inv = ROPE_THETA ** (-jnp.arange(n_freq, dtype=F32) / n_freq)
    ar = row[:, None] * inv
    ac = col[:, None] * inv
    cr, sr, cc, sc = jnp.cos(ar), jnp.sin(ar), jnp.cos(ac), jnp.sin(ac)
    cos = jnp.concatenate([cr, cr, cc, cc], axis=1)
    sin = jnp.concatenate([-sr, sr, -sc, sc], axis=1)
    return jnp.tile(cos, (1, 2)), jnp.tile(sin, (1, 2))


def _attn_kernel(*refs, nkv, has_ctx, post_scale):
    if has_ctx:
        lam_ref, g_ref, q_ref, k_ref, v_ref, kc_ref, vc_ref, o_ref, m1, l1, a1, m2, l2, a2 = refs
    else:
        lam_ref, g_ref, q_ref, k_ref, v_ref, o_ref, m1, l1, a1, m2, l2, a2 = refs
    kv = pl.program_id(2)

    q = q_ref[...]
    lane = lax.broadcasted_iota(jnp.int32, q.shape, 1)
    qs = q * jnp.asarray(A_HEAD_DIM ** -0.5, BF16)
    zero = jnp.zeros_like(qs)
    q1 = jnp.where(lane < A_HEAD_DIM, qs, zero)
    q2 = jnp.where(lane >= A_HEAD_DIM, qs, zero)

    @pl.when(kv == 0)
    def _():
        for m_ref, l_ref, a_ref in ((m1, l1, a1), (m2, l2, a2)):
            m_ref[...] = jnp.full(m_ref.shape, -jnp.inf, F32)
            l_ref[...] = jnp.zeros(l_ref.shape, F32)
            a_ref[...] = jnp.zeros(a_ref.shape, F32)

    def attend(k, v):
        reps = k.shape[0] // LANES
        for qq, m_ref, l_ref, a_ref in ((q1, m1, l1, a1), (q2, m2, l2, a2)):
            s = _dot_nt(qq, k)
            m_prev = m_ref[...]
            m_new = jnp.maximum(m_prev, jnp.max(s, axis=1, keepdims=True))
            alpha = jnp.exp(m_prev - m_new)
            p = jnp.exp(s - jnp.concatenate([m_new] * reps, axis=1))
            l_ref[...] = alpha * l_ref[...] + jnp.sum(p, axis=1, keepdims=True)
            a_ref[...] = alpha * a_ref[...] + _dot(p.astype(BF16), v)
            m_ref[...] = m_new

    if has_ctx:
        @pl.when(kv == 0)
        def _():
            attend(kc_ref[...], vc_ref[...])

    attend(k_ref[...], v_ref[...])

    @pl.when(kv == nkv - 1)
    def _():
        o = a1[...] / l1[...] - lam_ref[...] * (a2[...] / l2[...])
        o_ref[...] = (_rms(o, g_ref[...]) * post_scale).astype(o_ref.dtype)


def _attn(p_q, p_kv, p_ctx, lam, subln_g, post_scale):
    n = p_q.shape[0]
    nk = p_kv.shape[0]
    tq = _tile(n, 512)
    tk = _tile(nk, 512, LANES)
    nkv = nk // tk
    kb, vb = A_QK // LANES, 2 * A_QK // LANES
    vec = pl.BlockSpec((1, LANES), lambda h, i, kv: (0, 0))
    in_specs = [vec, vec,
                pl.BlockSpec((tq, LANES), lambda h, i, kv: (i, h)),
                pl.BlockSpec((tk, LANES), lambda h, i, kv: (kv, kb + h)),
                pl.BlockSpec((tk, LANES), lambda h, i, kv: (kv, vb + h))]
    args = [lam, subln_g, p_q, p_kv, p_kv]
    if p_ctx is not None:
        nc = p_ctx.shape[0]
        in_specs += [pl.BlockSpec((nc, LANES), lambda h, i, kv: (0, kb + h)),
                     pl.BlockSpec((nc, LANES), lambda h, i, kv: (0, vb + h))]
        args += [p_ctx, p_ctx]
    stat = pltpu.VMEM((tq, LANES), F32)
    return pl.pallas_call(
        functools.partial(_attn_kernel, nkv=nkv, has_ctx=p_ctx is not None, post_scale=post_scale),
        grid=(A_HEADS, n // tq, nkv),
        in_specs=in_specs,
        out_specs=pl.BlockSpec((tq, LANES), lambda h, i, kv: (i, h)),
        out_shape=jax.ShapeDtypeStruct((n, A_WIDTH), BF16),
        scratch_shapes=[stat] * 6,
        compiler_params=_cp("parallel", "parallel", "arbitrary"),
        name="diff_attn",
    )(*args)


def _pool_kernel(prev_ref, cur_ref, next_ref, w_ref, sc_ref, o_ref, *, n_tokens):
    tm = cur_ref.shape[0]
    ext_rows = tm + 2 * POOL_HALO
    base = pl.program_id(0) * tm - POOL_HALO
    ext = jnp.concatenate([prev_ref[...], cur_ref[...], next_ref[...]], axis=0).astype(F32)
    pos = base + lax.broadcasted_iota(jnp.int32, (ext_rows, 1), 0)
    ext = jnp.where((pos >= 0) & (pos < n_tokens), ext, 0.0)
    t = (pos[POOL_HALO:POOL_HALO + tm]).astype(F32)

    def shifted(a, d):
        return pltpu.roll(a, (-d) % ext_rows, 0)

    outs = []
    for gi, w in enumerate(POOL_WINDOWS):
        e = ext[:, gi * B_GROUP_DIM:(gi + 1) * B_GROUP_DIM]
        s = e + shifted(e, -1)
        half = 1
        while 2 * half < w:
            s = shifted(s, half) + shifted(s, -half)
            half *= 2
        lo = jnp.maximum(t - (w // 2), 0.0)
        hi = jnp.minimum(t + (w - 1 - w // 2), float(n_tokens - 1))
        cnt = hi - lo + 1.0
        pooled = s[POOL_HALO:POOL_HALO + tm] / cnt - e[POOL_HALO:POOL_HALO + tm]
        outs.append(_dot(pooled.astype(BF16), w_ref[gi]))
    o_ref[...] = (jnp.concatenate(outs, axis=1) * sc_ref[...]).astype(o_ref.dtype)


def _pool(p, w_pool, pool_scale):
    n = p.shape[0]
    tm = _tile(n, 512, POOL_HALO)
    ub = (p.shape[1] - B_WIDTH) // B_WIDTH
    hb = tm // POOL_HALO
    last = n // POOL_HALO - 1
    return pl.pallas_call(
        functools.partial(_pool_kernel, n_tokens=n),
        grid=(n // tm,),
        in_specs=[
            pl.BlockSpec((POOL_HALO, B_WIDTH), lambda i: (jnp.maximum(i * hb - 1, 0), ub)),
            pl.BlockSpec((tm, B_WIDTH), lambda i: (i, ub)),
            pl.BlockSpec((POOL_HALO, B_WIDTH), lambda i: (jnp.minimum((i + 1) * hb, last), ub)),
            pl.BlockSpec((B_GROUPS, B_GROUP_DIM, B_GROUP_DIM), lambda i: (0, 0, 0)),
            pl.BlockSpec((1, B_WIDTH), lambda i: (0, 0)),
        ],
        out_specs=pl.BlockSpec((tm, B_WIDTH), lambda i: (i, 0)),
        out_shape=jax.ShapeDtypeStruct((n, B_WIDTH), BF16),
        compiler_params=_cp("parallel"),
        name="pool",
    )(p, p, p, w_pool, pool_scale)


def _outres_kernel(*refs, n_in):
    a_refs, w_refs = refs[:n_in], refs[n_in:2 * n_in]
    x_ref, g_ref, gate_ref, o_ref = refs[2 * n_in:]
    acc = _dot(a_refs[0][...], w_refs[0][...])
    for a_ref, w_ref in zip(a_refs[1:], w_refs[1:]):
        acc = acc + _dot(a_ref[...], w_ref[...])
    o_ref[...] = x_ref[...] + gate_ref[...] * _rms(acc, g_ref[...])


def _outres(acts, ws, x, g, gate):
    m, d = x.shape
    tm = _tile(m, 256)
    vec = pl.BlockSpec((1, d), lambda i: (0, 0))
    in_specs = ([pl.BlockSpec((tm, a.shape[1]), lambda i: (i, 0)) for a in acts]
                + [pl.BlockSpec(w.shape, lambda i: (0, 0)) for w in ws]
                + [pl.BlockSpec((tm, d), lambda i: (i, 0)), vec, vec])
    return pl.pallas_call(
        functools.partial(_outres_kernel, n_in=len(acts)),
        grid=(m // tm,),
        in_specs=in_specs,
        out_specs=pl.BlockSpec((tm, d), lambda i: (i, 0)),
        out_shape=jax.ShapeDtypeStruct((m, d), F32),
        compiler_params=_cp("parallel"),
        name="out_residual",
    )(*acts, *ws, x, g, gate)


def _ffn_kernel(x_ref, g_ref, sh_ref, sc_ref, wg_ref, wu_ref, wd_ref, g2_ref, gate_ref,
                o_ref, h_scr, acc_scr, *, nf):
    f = pl.program_id(1)

    @pl.when(f == 0)
    def _():
        h_scr[...] = _norm_mod(x_ref[...], g_ref[...], sh_ref[...], sc_ref[...]).astype(BF16)
        acc_scr[...] = jnp.zeros(acc_scr.shape, F32)

    h = h_scr[...]
    a = (_silu(_dot(h, wg_ref[...])) * _dot(h, wu_ref[...])).astype(BF16)
    acc_scr[...] += _dot(a, wd_ref[...])

    @pl.when(f == nf - 1)
    def _():
        o_ref[...] = x_ref[...] + gate_ref[...] * _rms(acc_scr[...], g2_ref[...])


def _ffn(x, g, shift, scale, wg, wu, wd, g2, gate):
    m, d = x.shape
    ff = wg.shape[1]
    tm = _tile(m, 512)
    tf = _tile(ff, 512, LANES)
    nf = ff // tf
    vec = pl.BlockSpec((1, d), lambda i, f: (0, 0))
    return pl.pallas_call(
        functools.partial(_ffn_kernel, nf=nf),
        grid=(m // tm, nf),
        in_specs=[pl.BlockSpec((tm, d), lambda i, f: (i, 0)), vec, vec, vec,
                  pl.BlockSpec((d, tf), lambda i, f: (0, f)),
                  pl.BlockSpec((d, tf), lambda i, f: (0, f)),
                  pl.BlockSpec((tf, d), lambda i, f: (f, 0)),
                  vec, vec],
        out_specs=pl.BlockSpec((tm, d), lambda i, f: (i, 0)),
        out_shape=jax.ShapeDtypeStruct((m, d), F32),
        scratch_shapes=[pltpu.VMEM((tm, d), BF16), pltpu.VMEM((tm, d), F32)],
        compiler_params=_cp("parallel", "arbitrary"),
        name="ffn",
    )(x, g, shift, scale, wg, wu, wd, g2, gate)


def _gla_kernel(q_ref, k_ref, v_ref, gl_ref, w2_ref, b_ref, s0_ref, o_ref, sfin_ref, st_scr,
                *, reverse, nblk):
    blk = pl.program_id(1)
    rows = q_ref.shape[0]
    nch = rows // C_CHUNK

    @pl.when(blk == 0)
    def _():
        st_scr[...] = s0_ref[0]

    z = _dot(gl_ref[...], w2_ref[...]) + b_ref[...]
    g = -(jnp.maximum(-z, 0.0) + jnp.log1p(jnp.exp(-jnp.abs(z)))) * (1.0 / C_TAU)

    r_i = lax.broadcasted_iota(jnp.int32, (rows, rows), 0)
    c_i = lax.broadcasted_iota(jnp.int32, (rows, rows), 1)
    shift = C_CHUNK.bit_length() - 1
    same = (r_i >> shift) == (c_i >> shift)
    tri = same & ((c_i >= r_i) if reverse else (c_i <= r_i))
    g3 = _split3(g)
    tri_m = jnp.where(tri, 1.0, 0.0).astype(BF16)
    same_m = jnp.where(same, 1.0, 0.0).astype(BF16)
    bc = _dot(tri_m, g3[0]) + _dot(tri_m, g3[1]) + _dot(tri_m, g3[2])
    tot = _dot(same_m, g3[0]) + _dot(same_m, g3[1]) + _dot(same_m, g3[2])

    q = q_ref[...].astype(F32) * (C_DK ** -0.5)
    k = k_ref[...].astype(F32)
    q_dec = (q * jnp.exp(bc)).astype(BF16)
    k_inv = (k * jnp.exp(-bc)).astype(BF16)
    k_end = (k * jnp.exp(tot - bc)).astype(BF16)
    dec = jnp.exp(tot)
    v = v_ref[...]

    r64 = lax.broadcasted_iota(jnp.int32, (C_CHUNK, C_CHUNK), 0)
    c64 = lax.broadcasted_iota(jnp.int32, (C_CHUNK, C_CHUNK), 1)
    mask = (c64 > r64) if reverse else (c64 <= r64)

    order = range(nch - 1, -1, -1) if reverse else range(nch)
    for c in order:
        sl = slice(c * C_CHUNK, (c + 1) * C_CHUNK)
        st = st_scr[...]
        att = jnp.where(mask, _dot_nt(q_dec[sl], k_inv[sl]), 0.0).astype(BF16)
        o_ref[sl, :] = _dot(att, v[sl]) + _dot_nt(q_dec[sl], st.astype(BF16))
        st_scr[...] = st * dec[c * C_CHUNK:c * C_CHUNK + 1, :] + _dot_tn(v[sl], k_end[sl])

    @pl.when(blk == nblk - 1)
    def _():
        sfin_ref[0] = st_scr[...]


def _gla_scan(p, w2pad, gate_b, s0, reverse):
    n = p.shape[0]
    rows = GLA_ROWS
    nblk = n // rows
    kb = C_NQ // C_DK
    vb = 2 * C_NQ // C_DV
    glb = (2 * C_NQ + 2 * C_NV) // LANES

    def blk(b):
        return (nblk - 1 - b) if reverse else b

    return pl.pallas_call(
        functools.partial(_gla_kernel, reverse=reverse, nblk=nblk),
        grid=(C_HEADS, nblk),
        in_specs=[
            pl.BlockSpec((rows, C_DK), lambda h, b: (blk(b), h)),
            pl.BlockSpec((rows, C_DK), lambda h, b: (blk(b), kb + h)),
            pl.BlockSpec((rows, C_DV), lambda h, b: (blk(b), vb + h)),
            pl.BlockSpec((rows, LANES), lambda h, b: (blk(b), glb)),
            pl.BlockSpec((LANES, C_DK), lambda h, b: (0, h)),
            pl.BlockSpec((1, C_DK), lambda h, b: (0, h)),
            pl.BlockSpec((1, C_DV, C_DK), lambda h, b: (h, 0, 0)),
        ],
        out_specs=[
            pl.BlockSpec((rows, C_DV), lambda h, b: (blk(b), h)),
            pl.BlockSpec((1, C_DV, C_DK), lambda h, b: (h, 0, 0)),
        ],
        out_shape=[jax.ShapeDtypeStruct((n, C_NV), F32),
                   jax.ShapeDtypeStruct((C_HEADS, C_DV, C_DK), F32)],
        scratch_shapes=[pltpu.VMEM((C_DV, C_DK), F32)],
        compiler_params=_cp("parallel", "arbitrary"),
        name="gla_scan_bwd" if reverse else "gla_scan_fwd",
    )(p, p, p, p, w2pad, gate_b, s0)


def _glaout_kernel(of_ref, ob_ref, r_ref, ng_ref, w_ref, x_ref, g_ref, gate_ref, o_ref):
    o = of_ref[...] + ob_ref[...]
    ng = ng_ref[...]
    heads = [_rms(o[:, h * C_DV:(h + 1) * C_DV], ng[:, h * C_DV:(h + 1) * C_DV])
             for h in range(C_HEADS)]
    a = (jnp.concatenate(heads, axis=1) * _silu(r_ref[...].astype(F32))).astype(BF16)
    o_ref[...] = x_ref[...] + gate_ref[...] * _rms(_dot(a, w_ref[...]), g_ref[...])


def _glaout(o_f, o_b, p, norm_g, w_out, x, g, gate):
    m, d = x.shape
    tm = _tile(m, 256)
    rb = (2 * C_NQ + C_NV) // C_NV
    vec = pl.BlockSpec((1, d), lambda i: (0, 0))
    row = pl.BlockSpec((tm, C_NV), lambda i: (i, 0))
    return pl.pallas_call(
        _glaout_kernel,
        grid=(m // tm,),
        in_specs=[row, row,
                  pl.BlockSpec((tm, C_NV), lambda i: (i, rb)),
                  pl.BlockSpec((1, C_NV), lambda i: (0, 0)),
                  pl.BlockSpec(w_out.shape, lambda i: (0, 0)),
                  pl.BlockSpec((tm, d), lambda i: (i, 0)), vec, vec],
        out_specs=pl.BlockSpec((tm, d), lambda i: (i, 0)),
        out_shape=jax.ShapeDtypeStruct((m, d), F32),
        compiler_params=_cp("parallel"),
        name="gla_out",
    )(o_f, o_b, p, norm_g, w_out, x, g, gate)


def _router_kernel(x_ref, g_ref, sh_ref, sc_ref, wr_ref, h_ref, lg_ref):
    h = _norm_mod(x_ref[...], g_ref[...], sh_ref[...], sc_ref[...])
    h_ref[...] = h
    lg_ref[...] = _dot_f32(h, wr_ref[...])


def _router(x, g, shift, scale, w_router_pad):
    m, d = x.shape
    tm = _tile(m, 256)
    vec = pl.BlockSpec((1, d), lambda i: (0, 0))
    return pl.pallas_call(
        _router_kernel,
        grid=(m // tm,),
        in_specs=[pl.BlockSpec((tm, d), lambda i: (i, 0)), vec, vec, vec,
                  pl.BlockSpec((d, LANES), lambda i: (0, 0))],
        out_specs=[pl.BlockSpec((tm, d), lambda i: (i, 0)),
                   pl.BlockSpec((tm, LANES), lambda i: (i, 0))],
        out_shape=[jax.ShapeDtypeStruct((m, d), F32),
                   jax.ShapeDtypeStruct((m, LANES), F32)],
        compiler_params=_cp("parallel"),
        name="router",
    )(x, g, shift, scale, w_router_pad)


def _row_gather(idx_ref, n_rows, src_hbm, dst, sem, dst_row):
    def copy(r):
        return pltpu.make_async_copy(src_hbm.at[pl.ds(idx_ref[0, 0, r], 1)], dst_row(dst, r), sem)

    def start(r, carry):
        copy(r).start()
        return carry

    def wait(r, carry):
        copy(r).wait()
        return carry

    lax.fori_loop(0, n_rows, start, 0)
    lax.fori_loop(0, n_rows, wait, 0)


def _moe_kernel(te_ref, tv_ref, rows_ref, h_hbm, wg_ref, wu_ref, wd_ref, y_ref,
                hrow, hb, acc, sem, *, nf):
    t = pl.program_id(0)
    f = pl.program_id(1)
    valid = tv_ref[t] == 1
    tm = hrow.shape[0]

    @pl.when(f == 0)
    def _():
        acc[...] = jnp.zeros(acc.shape, F32)

    @pl.when(valid & (f == 0))
    def _():
        _row_gather(rows_ref, tm, h_hbm, hrow, sem.at[0], lambda dst, r: dst.at[pl.ds(r, 1)])
        hb[...] = hrow[...].astype(BF16)

    @pl.when(valid)
    def _():
        h = hb[...]
        a = (_silu(_dot(h, wg_ref[0])) * _dot(h, wu_ref[0])).astype(BF16)
        acc[...] += _dot(a, wd_ref[0])

    @pl.when(f == nf - 1)
    def _():
        y_ref[...] = acc[...]


def _moe_ffn(h, tile_expert, tile_valid, row_token, wg, wu, wd, tm):
    n_tiles = tile_expert.shape[0]
    d = h.shape[1]
    ff = wg.shape[2]
    tf = _tile(ff, 512, LANES)
    nf = ff // tf

    def fblk(t, f, tv):
        return jnp.where(tv[t] == 1, f, nf - 1)

    grid_spec = pltpu.PrefetchScalarGridSpec(
        num_scalar_prefetch=2,
        grid=(n_tiles, nf),
        in_specs=[
            pl.BlockSpec((1, 1, tm), lambda t, f, te, tv: (t, 0, 0), memory_space=pltpu.SMEM),
            pl.BlockSpec(memory_space=pl.ANY),
            pl.BlockSpec((1, d, tf), lambda t, f, te, tv: (te[t], 0, fblk(t, f, tv))),
            pl.BlockSpec((1, d, tf), lambda t, f, te, tv: (te[t], 0, fblk(t, f, tv))),
            pl.BlockSpec((1, tf, d), lambda t, f, te, tv: (te[t], fblk(t, f, tv), 0)),
        ],
        out_specs=pl.BlockSpec((tm, d), lambda t, f, te, tv: (t, 0)),
        scratch_shapes=[pltpu.VMEM((tm, d), F32), pltpu.VMEM((tm, d), BF16),
                        pltpu.VMEM((tm, d), F32), pltpu.SemaphoreType.DMA((1,))],
    )
    return pl.pallas_call(
        functools.partial(_moe_kernel, nf=nf),
        grid_spec=grid_spec,
        out_shape=jax.ShapeDtypeStruct((n_tiles * tm, d), F32),
        compiler_params=_cp("arbitrary", "arbitrary"),
        name="moe_ffn",
    )(tile_expert, tile_valid, row_token.reshape(n_tiles, 1, tm), h, wg, wu, wd)


def _combine_kernel(pos_ref, y_hbm, wt_ref, x_ref, g_ref, gate_ref, o_ref, ybuf, sem):
    tm = x_ref.shape[0]
    _row_gather(pos_ref, TOP_K * tm, y_hbm, ybuf, sem.at[0],
                lambda dst, r: dst.at[r % TOP_K, pl.ds(r // TOP_K, 1)])
    wt = wt_ref[...]
    y = wt[:, 0:1] * ybuf[0] + wt[:, 1:2] * ybuf[1]
    o_ref[...] = x_ref[...] + gate_ref[...] * _rms(y, g_ref[...])


def _moe_combine(y_sorted, pos, top_w_pad, x, g, gate):
    m, d = x.shape
    tm = _tile(m, 256)
    vec = pl.BlockSpec((1, d), lambda i: (0, 0))
    return pl.pallas_call(
        _combine_kernel,
        grid=(m // tm,),
        in_specs=[
            pl.BlockSpec((1, 1, TOP_K * tm), lambda i: (i, 0, 0), memory_space=pltpu.SMEM),
            pl.BlockSpec(memory_space=pl.ANY),
            pl.BlockSpec((tm, LANES), lambda i: (i, 0)),
            pl.BlockSpec((tm, d), lambda i: (i, 0)), vec, vec],
        out_specs=pl.BlockSpec((tm, d), lambda i: (i, 0)),
        out_shape=jax.ShapeDtypeStruct((m, d), F32),
        scratch_shapes=[pltpu.VMEM((TOP_K, tm, d), F32), pltpu.SemaphoreType.DMA((1,))],
        compiler_params=_cp("arbitrary"),
        name="moe_combine",
    )(pos.reshape(m // tm, 1, TOP_K * tm), y_sorted, top_w_pad, x, g, gate)


def _route(logits, tm):
    n = logits.shape[0]
    top_v, top_i = lax.top_k(logits[:, :N_EXPERTS], TOP_K)
    top_w = jax.nn.softmax(top_v, axis=-1)
    e_flat = top_i.reshape(-1).astype(jnp.int32)
    n_ent = n * TOP_K
    n_tiles = (n_ent + N_EXPERTS * (tm - 1)) // tm
    order = jnp.argsort(e_flat, stable=True).astype(jnp.int32)
    e_sorted = e_flat[order]
    counts = jnp.sum(e_flat[:, None] == jnp.arange(N_EXPERTS)[None, :], axis=0).astype(jnp.int32)
    padded = ((counts + tm - 1) // tm) * tm
    pend = jnp.cumsum(padded)
    pstart = pend - padded
    start = jnp.cumsum(counts) - counts
    pos_sorted = pstart[e_sorted] + (jnp.arange(n_ent, dtype=jnp.int32) - start[e_sorted])
    row_token = jnp.zeros((n_tiles * tm,), jnp.int32).at[pos_sorted].set(order // TOP_K)
    pos = jnp.zeros((n_ent,), jnp.int32).at[order].set(pos_sorted)
    tile_start = jnp.arange(n_tiles, dtype=jnp.int32) * tm
    tile_valid = (tile_start < pend[-1]).astype(jnp.int32)
    last_valid = jnp.maximum(pend[-1] - 1, 0)
    tile_expert = jnp.searchsorted(pend, jnp.minimum(tile_start, last_valid), side="right")
    tile_expert = jnp.minimum(tile_expert, N_EXPERTS - 1).astype(jnp.int32)
    top_w_pad = jnp.pad(top_w.astype(F32), ((0, 0), (0, LANES - TOP_K)))
    return tile_expert, tile_valid, row_token, pos, top_w_pad


def _moe_layer(x, g2, shift, scale, w_router_pad, wg, wu, wd, g3, gate, tm):
    h, logits = _router(x, g2, shift, scale, w_router_pad)
    tile_expert, tile_valid, row_token, pos, top_w_pad = _route(logits, tm)
    y_sorted = _moe_ffn(h, tile_expert, tile_valid, row_token, wg, wu, wd, tm)
    return _moe_combine(y_sorted, pos, top_w_pad, x, g3, gate)


def kernel(x, c, ctx, c_ctx, w_mod, b_mod, norm_g, w_in_a, w_out_a, diff_lambda, subln_g, w_pool,
           pool_scale, w_in_c, gate_w2, gate_b, gla_norm, w_out_c, w_ff_gate, w_ff_up, w_ff_down,
           w_router, w_moe_gate, w_moe_up, w_moe_down):
    assert x.shape[0] == 1 and ctx.shape[0] == 1
    depth = w_mod.shape[0]
    d = x.shape[-1]
    x = x[0]
    ctx = ctx[0]
    n = x.shape[0]
    assert n % GRID_W == 0 and n % GLA_ROWS == 0 and ctx.shape[0] % GLA_ROWS == 0

    cond = jnp.zeros((16, d), F32).at[0].set(c[0]).at[1].set(c_ctx)
    mods = _modvec(cond, w_mod, b_mod)
    rope = _rope_tables(n)

    for l in range(depth):
        last = l == depth - 1
        i = l // 2
        ml = mods[l, 0].reshape(6, 1, d)
        mc = mods[l, 1].reshape(6, 1, d)
        g = norm_g[l].reshape(4, 1, d)
        if l % 2 == 0:
            lam_init = 0.8 - 0.6 * math.exp(-0.3 * l)
            lp = diff_lambda[i].astype(F32)
            lam = jnp.exp(jnp.sum(lp[0] * lp[1])) - jnp.exp(jnp.sum(lp[2] * lp[3])) + lam_init
            lam_vec = jnp.full((1, LANES), lam, F32)
            post = 1.0 - lam_init
            w_in = w_in_a[i].astype(BF16)
            w_out = w_out_a[i].astype(BF16)
            w_pl = w_pool[i].astype(BF16)
            sub_g = subln_g[i].reshape(1, A_V_DIM)
            p_sc = pool_scale[i].reshape(1, B_WIDTH)
            wg, wu, wd = (w_ff_gate[i].astype(BF16), w_ff_up[i].astype(BF16),
                          w_ff_down[i].astype(BF16))

            p_lat = _proj(x, g[0], ml[0], ml[1], w_in, rope=rope, rope_cols=2 * A_QK)
            p_ctx = _proj(ctx, g[0], mc[0], mc[1], w_in)
            a_lat = _attn(p_lat, p_lat, p_ctx, lam_vec, sub_g, post)
            b_lat = _pool(p_lat, w_pl, p_sc)
            x = _outres([a_lat, b_lat], [w_out[:A_WIDTH], w_out[A_WIDTH:]], x, g[1], ml[2])
            x = _ffn(x, g[2], ml[3], ml[4], wg, wu, wd, g[3], ml[5])
            if not last:
                a_ctx = _attn(p_ctx, p_ctx, None, lam_vec, sub_g, post)
                b_ctx = _pool(p_ctx, w_pl, p_sc)
                ctx = _outres([a_ctx, b_ctx], [w_out[:A_WIDTH], w_out[A_WIDTH:]], ctx, g[1], mc[2])
                ctx = _ffn(ctx, g[2], mc[3], mc[4], wg, wu, wd, g[3], mc[5])
        else:
            odd_in = w_in_c.shape[-1]
            pad_cols = (-odd_in) % (7 * LANES)
            w_in = jnp.pad(w_in_c[i], ((0, 0), (0, pad_cols))).astype(BF16)
            w_out = w_out_c[i].astype(BF16)
            w2pad = [jnp.zeros((LANES, C_NQ), F32)
                     .at[dd * C_GATE_RANK:(dd + 1) * C_GATE_RANK].set(gate_w2[i, dd]).astype(BF16)
                     for dd in range(2)]
            gb = [gate_b[i, dd].reshape(1, C_NQ) for dd in range(2)]
            ng = gla_norm[i].reshape(1, C_NV)
            w_r = jnp.pad(w_router[i], ((0, 0), (0, LANES - N_EXPERTS)))
            wg, wu, wd = (w_moe_gate[i].astype(BF16), w_moe_up[i].astype(BF16),
                          w_moe_down[i].astype(BF16))

            p_ctx = _proj(ctx, g[0], mc[0], mc[1], w_in, tn_pref=896)
            p_lat = _proj(x, g[0], ml[0], ml[1], w_in, tn_pref=896)
            s0 = jnp.zeros((C_HEADS, C_DV, C_DK), F32)
            o_cf, s_cf = _gla_scan(p_ctx, w2pad[0], gb[0], s0, False)
            o_cb, s_cb = _gla_scan(p_ctx, w2pad[1], gb[1], s0, True)
            o_lf, _ = _gla_scan(p_lat, w2pad[0], gb[0], s_cf, False)
            o_lb, _ = _gla_scan(p_lat, w2pad[1], gb[1], s_cb, True)
            x = _glaout(o_lf, o_lb, p_lat, ng, w_out, x, g[1], ml[2])
            x = _moe_layer(x, g[2], ml[3], ml[4], w_r, wg, wu, wd, g[3], ml[5], tm=512)
            if not last:
                ctx = _glaout(o_cf, o_cb, p_ctx, ng, w_out, ctx, g[1], mc[2])
                ctx = _moe_layer(ctx, g[2], mc[3], mc[4], w_r, wg, wu, wd, g[3], mc[5], tm=128)
    return x[None]
```
